```python
import jax, jax.numpy as jnp
from jax import lax
import numpy as np

D_MODEL = 1024
BATCH = 2
SEQ = 8192
DEPTH = 1

D_MIX = D_MODEL
D_MLSTM = D_MIX // 2
D_CONV = D_MIX - D_MLSTM
N_MLSTM_HEADS = 4
HEAD_DIM = D_MLSTM // N_MLSTM_HEADS
N_CONV_GROUPS = 4
CONV_GROUP_DIM = D_CONV // N_CONV_GROUPS
CONV_WIDTH = 31
QK_CONV_WIDTH = 4
CHUNK = 128
D_FF = 4 * D_MODEL
EPS = 1e-6
D_IN_PROJ = 4 * D_MLSTM + 2 * N_MLSTM_HEADS + 2 * D_CONV

kernel_name = "hybrid_mlstm_conformerconv_sqrelu_sandwich"


def rms_norm(x, w):
    xf = x.astype(jnp.float32)
    y = xf * lax.rsqrt(jnp.mean(xf * xf, axis=-1, keepdims=True) + EPS)
    return (y * w.astype(jnp.float32)).astype(x.dtype)


def causal_depthwise_conv(x, w, b):
    width, ch = w.shape
    y = lax.conv_general_dilated(
        x, w[:, None, :].astype(x.dtype), window_strides=(1,), padding=((width - 1, 0),),
        dimension_numbers=("NWC", "WIO", "NWC"), feature_group_count=ch)
    return y + b.astype(x.dtype)


def mlstm_chunkwise(q, k, v, i_pre, f_pre):
    B, H, S, dh = q.shape
    nc = S // CHUNK

    def to_chunks(t):
        t = t.reshape((B, H, nc, CHUNK) + t.shape[3:])
        return jnp.moveaxis(t, 2, 0)

    q = q * (dh ** -0.5)
    log_f = jax.nn.log_sigmoid(f_pre)
    b_cum = jnp.cumsum(to_chunks(log_f), axis=-1)
    causal = jnp.tril(jnp.ones((CHUNK, CHUNK), dtype=bool))

    def step(carry, xs):
        C, n, m = carry
        qc, kc, vc, li, bc = xs
        inter = bc + m[..., None]
        D = bc[..., :, None] - bc[..., None, :] + li[..., None, :]
        D = jnp.where(causal, D, -jnp.inf)
        m_row = jnp.maximum(inter, jnp.max(D, axis=-1))
        w_inter = jnp.exp(inter - m_row)
        P = jnp.einsum("bhld,bhsd->bhls", qc, kc) * jnp.exp(D - m_row[..., None])
        num = w_inter[..., None] * jnp.einsum("bhld,bhde->bhle", qc, C) + jnp.einsum("bhls,bhse->bhle", P, vc)
        den = w_inter * jnp.einsum("bhld,bhd->bhl", qc, n) + jnp.sum(P, axis=-1)
        h = num / jnp.maximum(jnp.abs(den), jnp.exp(-m_row))[..., None]
        b_last = bc[..., -1]
        g = b_last[..., None] - bc + li
        m_new = jnp.maximum(b_last + m, jnp.max(g, axis=-1))
        decay = jnp.exp(b_last + m - m_new)
        wk = jnp.exp(g - m_new[..., None])[..., None] * kc
        C_new = decay[..., None, None] * C + jnp.einsum("bhsd,bhse->bhde", wk, vc)
        n_new = decay[..., None] * n + jnp.sum(wk, axis=-2)
        return (C_new, n_new, m_new), h

    init = (jnp.zeros((B, H, dh, dh), jnp.float32), jnp.zeros((B, H, dh), jnp.float32),
            jnp.zeros((B, H), jnp.float32))
    xs = (to_chunks(q), to_chunks(k), to_chunks(v), to_chunks(i_pre), b_cum)
    _, h = lax.scan(step, init, xs)
    return jnp.moveaxis(h, 0, 2).reshape(B, H, S, dh)


def setup_inputs(seed: int = 0) -> dict:
    key = jax.random.key(seed)
    ks = jax.random.split(key, 20)
    f32 = jnp.float32

    def gain(k, shape):
        return (1.0 + 0.1 * jax.random.normal(k, shape)).astype(f32)

    def dense(k, shape, fan_in):
        return (jax.random.normal(k, shape) * fan_in ** -0.5).astype(f32)

    f_bias = jnp.linspace(3.0, 6.0, N_MLSTM_HEADS)[None, :] + 0.1 * jax.random.normal(ks[4], (DEPTH, N_MLSTM_HEADS))
    i_bias = 0.1 * jax.random.normal(ks[5], (DEPTH, N_MLSTM_HEADS))
    return {
        "x": jax.random.normal(ks[0], (BATCH, SEQ, D_MODEL), f32),
        "pre_mix_norm_w": gain(ks[1], (DEPTH, D_MODEL)),
        "w_in": dense(ks[2], (DEPTH, D_MODEL, D_IN_PROJ), D_MODEL),
        "qk_conv_w": dense(ks[3], (DEPTH, QK_CONV_WIDTH, 2 * D_MLSTM), QK_CONV_WIDTH),
        "qk_conv_b": (0.02 * jax.random.normal(ks[6], (DEPTH, 2 * D_MLSTM))).astype(f32),
        "gate_bias": jnp.concatenate([i_bias, f_bias], axis=-1).astype(f32),
        "mlstm_norm_w": gain(ks[7], (DEPTH, D_MLSTM)),
        "conv_w": dense(ks[8], (DEPTH, CONV_WIDTH, D_CONV), CONV_WIDTH),
        "conv_b": (0.02 * jax.random.normal(ks[9], (DEPTH, D_CONV))).astype(f32),
        "conv_norm_w": gain(ks[10], (DEPTH, D_CONV)),
        "conv_norm_b": (0.02 * jax.random.normal(ks[11], (DEPTH, D_CONV))).astype(f32),
        "w_out": dense(ks[12], (DEPTH, D_MIX, D_MODEL), D_MIX),
        "post_mix_norm_w": gain(ks[13], (DEPTH, D_MODEL)),
        "pre_mlp_norm_w": gain(ks[14], (DEPTH, D_MODEL)),
        "w_ff1": dense(ks[15], (DEPTH, D_MODEL, D_FF), D_MODEL),
        "w_ff2": dense(ks[16], (DEPTH, D_FF, D_MODEL), D_FF),
        "post_mlp_norm_w": gain(ks[17], (DEPTH, D_MODEL)),
    }


def reference(x, pre_mix_norm_w, w_in, qk_conv_w, qk_conv_b, gate_bias, mlstm_norm_w, conv_w, conv_b,
              conv_norm_w, conv_norm_b, w_out, post_mix_norm_w, pre_mlp_norm_w, w_ff1, w_ff2, post_mlp_norm_w):
    B, S, _ = x.shape
    H, dh = N_MLSTM_HEADS, HEAD_DIM
    for layer in range(DEPTH):
        h_in = rms_norm(x, pre_mix_norm_w[layer])
        proj = jnp.einsum("bsd,de->bse", h_in, w_in[layer])
        o1 = 2 * D_MLSTM
        o2 = o1 + D_MLSTM
        o3 = o2 + D_MLSTM
        o4 = o3 + 2 * H
        o5 = o4 + D_CONV
        qk_raw, v_raw, o_pre, gates, conv_val, conv_gate = (
            proj[..., :o1], proj[..., o1:o2], proj[..., o2:o3], proj[..., o3:o4], proj[..., o4:o5], proj[..., o5:])

        qk = jax.nn.silu(causal_depthwise_conv(qk_raw, qk_conv_w[layer], qk_conv_b[layer]))
        q, k = qk[..., :D_MLSTM], qk[..., D_MLSTM:]
        gates = gates.astype(jnp.float32) + gate_bias[layer].astype(jnp.float32)
        i_pre = jnp.transpose(gates[..., :H], (0, 2, 1))
        f_pre = jnp.transpose(gates[..., H:], (0, 2, 1))

        def heads(t):
            return jnp.transpose(t.reshape(B, S, H, dh), (0, 2, 1, 3)).astype(jnp.float32)

        h_tilde = mlstm_chunkwise(heads(q), heads(k), heads(v_raw), i_pre, f_pre)
        mu = jnp.mean(h_tilde, axis=-1, keepdims=True)
        var = jnp.mean(jnp.square(h_tilde - mu), axis=-1, keepdims=True)
        h_norm = (h_tilde - mu) * lax.rsqrt(var + EPS)
        h_norm = jnp.transpose(h_norm, (0, 2, 1, 3)).reshape(B, S, D_MLSTM)
        y_mlstm = (jax.nn.sigmoid(o_pre.astype(jnp.float32)) * h_norm
                   * mlstm_norm_w[layer].astype(jnp.float32)).astype(x.dtype)

        u = conv_val * jax.nn.sigmoid(conv_gate)
        u = causal_depthwise_conv(u, conv_w[layer], conv_b[layer])
        uf = u.astype(jnp.float32).reshape(B, S, N_CONV_GROUPS, CONV_GROUP_DIM)
        mu_c = jnp.mean(uf, axis=-1, keepdims=True)
        var_c = jnp.mean(jnp.square(uf - mu_c), axis=-1, keepdims=True)
        uf = ((uf - mu_c) * lax.rsqrt(var_c + EPS)).reshape(B, S, D_CONV)
        uf = uf * conv_norm_w[layer].astype(jnp.float32) + conv_norm_b[layer].astype(jnp.float32)
        y_conv = jax.nn.silu(uf).astype(x.dtype)

        y = jnp.einsum("bse,ed->bsd", jnp.concatenate([y_mlstm, y_conv], axis=-1), w_out[layer])
        x = x + rms_norm(y, post_mix_norm_w[layer])

        h2 = rms_norm(x, pre_mlp_norm_w[layer])
        a = jax.nn.relu(jnp.einsum("bsd,df->bsf", h2, w_ff1[layer]))
        m_out = jnp.einsum("bsf,fd->bsd", a * a, w_ff2[layer])
        x = x + rms_norm(m_out, post_mlp_norm_w[layer])
    return x
```

```python
import functools

import jax
import jax.numpy as jnp
from jax import lax
from jax.experimental import pallas as pl
from jax.experimental.pallas import tpu as pltpu

F32 = jnp.float32
BF16 = jnp.bfloat16

N_HEADS = 4
HEAD_DIM = 128
N_CONV_GROUPS = 4
CONV_GROUP_DIM = 128
CHUNK = 128
EPS = 1e-6

TILE_IN = 512
TILE_MLSTM = 512
TILE_OUT = 512
FF_BLOCK = 1024
VMEM_LIMIT = 56 * 1024 * 1024


def _sigmoid(x):
    return 1.0 / (1.0 + jnp.exp(-x))


def _log_sigmoid(x):
    return jnp.minimum(x, 0.0) - jnp.log1p(jnp.exp(-jnp.abs(x)))


def _rms_scale(x):
    return lax.rsqrt(jnp.mean(x * x, axis=-1, keepdims=True) + EPS)


def _const_spec(shape):
    return pl.BlockSpec(shape, lambda *_: (0,) * len(shape), pipeline_mode=pl.Buffered(1))


def _in_proj_body(x_ref, nw_ref, w_ref, wgc_ref, wgr_ref, qkw_ref, qkb_ref, gbc_ref, gbr_ref,
                  cw_ref, cb_ref, cnw_ref, cnb_ref,
                  q_ref, k_ref, v_ref, o_ref, yc_ref, gcol_ref, grow_ref,
                  qkbuf, ubuf, *, tile, d_mlstm, d_conv, qk_width, conv_width):
    qk_halo = 8
    u_halo = 32
    d_qk = 2 * d_mlstm

    @pl.when(pl.program_id(1) == 0)
    def _():
        qkbuf[0:qk_halo, :] = jnp.zeros((qk_halo, d_qk), F32)
        ubuf[0:u_halo, :] = jnp.zeros((u_halo, d_conv), F32)

    x = x_ref[...]
    h = (x * _rms_scale(x) * nw_ref[...]).astype(BF16)

    qkbuf[qk_halo:qk_halo + tile, :] = jnp.dot(h, w_ref[:, 0:d_qk], preferred_element_type=F32)
    acc = qkb_ref[...]
    for j in range(qk_width):
        off = qk_halo - (qk_width - 1) + j
        acc = acc + qkw_ref[j:j + 1, :] * qkbuf[off:off + tile, :]
    qk = acc * _sigmoid(acc)
    q_ref[...] = (qk[:, :d_mlstm] * (HEAD_DIM ** -0.5)).astype(BF16)
    k_ref[...] = qk[:, d_mlstm:].astype(BF16)
    qkbuf[0:qk_halo, :] = qkbuf[tile:tile + qk_halo, :]

    c0 = d_qk
    v_ref[...] = jnp.dot(h, w_ref[:, c0:c0 + d_mlstm], preferred_element_type=F32).astype(BF16)
    c0 += d_mlstm
    o_ref[...] = jnp.dot(h, w_ref[:, c0:c0 + d_mlstm], preferred_element_type=F32)
    c0 += d_mlstm

    val = jnp.dot(h, w_ref[:, c0:c0 + d_conv], preferred_element_type=F32)
    gate = jnp.dot(h, w_ref[:, c0 + d_conv:c0 + 2 * d_conv], preferred_element_type=F32)
    ubuf[u_halo:u_halo + tile, :] = val * _sigmoid(gate)
    rows = 64
    for g in range(N_CONV_GROUPS):
        lanes = slice(g * CONV_GROUP_DIM, (g + 1) * CONV_GROUP_DIM)
        for r in range(tile // rows):
            acc = jnp.broadcast_to(cb_ref[:, lanes], (rows, CONV_GROUP_DIM))
            for j in range(conv_width):
                off = u_halo - (conv_width - 1) + j + r * rows
                acc = acc + cw_ref[j:j + 1, lanes] * ubuf[off:off + rows, lanes]
            mu = jnp.mean(acc, axis=-1, keepdims=True)
            cen = acc - mu
            var = jnp.mean(cen * cen, axis=-1, keepdims=True)
            y = cen * lax.rsqrt(var + EPS) * cnw_ref[:, lanes] + cnb_ref[:, lanes]
            yc_ref[r * rows:(r + 1) * rows, lanes] = (y * _sigmoid(y)).astype(BF16)
    ubuf[0:u_halo, :] = ubuf[tile:tile + u_halo, :]

    gc = jnp.dot(h, wgc_ref[...], preferred_element_type=F32) + gbc_ref[...]
    gr = lax.dot_general(wgr_ref[...], h, (((1,), (1,)), ((), ())),
                         preferred_element_type=F32) + gbr_ref[...]
    lf_c = _log_sigmoid(gc)
    lf_r = _log_sigmoid(gr)
    ri = lax.broadcasted_iota(jnp.int32, (CHUNK, CHUNK), 0)
    ci = lax.broadcasted_iota(jnp.int32, (CHUNK, CHUNK), 1)
    lower = (ci <= ri).astype(F32)
    upper = (ri <= ci).astype(F32)
    col_is_f = lax.broadcasted_iota(jnp.int32, (CHUNK, 128), 1) >= N_HEADS
    row_is_f = lax.broadcasted_iota(jnp.int32, (8, CHUNK), 0) >= N_HEADS
    for c in range(tile // CHUNK):
        sl = slice(c * CHUNK, (c + 1) * CHUNK)
        bc = jnp.dot(lower, lf_c[sl, :], preferred_element_type=F32, precision=lax.Precision.HIGHEST)
        gcol_ref[sl, :] = jnp.where(col_is_f, bc, gc[sl, :])[:, 0:2 * N_HEADS]
        br = jnp.dot(lf_r[:, sl], upper, preferred_element_type=F32, precision=lax.Precision.HIGHEST)
        grow_ref[:, sl] = jnp.where(row_is_f, br, gr[:, sl])


def _in_proj(x, nw, w_main, wgc, wgr, qkw, qkb, gbc, gbr, cw, cb, cnw, cnb):
    B, S, D = x.shape
    d_mlstm = N_HEADS * HEAD_DIM
    d_conv = N_CONV_GROUPS * CONV_GROUP_DIM
    tile = TILE_IN
    tok = lambda width: pl.BlockSpec((None, tile, width), lambda b, s: (b, s, 0))
    body = functools.partial(_in_proj_body, tile=tile, d_mlstm=d_mlstm, d_conv=d_conv,
                             qk_width=qkw.shape[0], conv_width=cw.shape[0])
    return pl.pallas_call(
        body,
        grid=(B, S // tile),
        in_specs=[tok(D)] + [_const_spec(a.shape) for a in
                             (nw, w_main, wgc, wgr, qkw, qkb, gbc, gbr, cw, cb, cnw, cnb)],
        out_specs=[tok(d_mlstm), tok(d_mlstm), tok(d_mlstm), tok(d_mlstm), tok(d_conv),
                   tok(2 * N_HEADS),
                   pl.BlockSpec((None, 2 * N_HEADS, tile), lambda b, s: (b, 0, s))],
        out_shape=[jax.ShapeDtypeStruct((B, S, d_mlstm), BF16),
                   jax.ShapeDtypeStruct((B, S, d_mlstm), BF16),
                   jax.ShapeDtypeStruct((B, S, d_mlstm), BF16),
                   jax.ShapeDtypeStruct((B, S, d_mlstm), F32),
                   jax.ShapeDtypeStruct((B, S, d_conv), BF16),
                   jax.ShapeDtypeStruct((B, S, 2 * N_HEADS), F32),
                   jax.ShapeDtypeStruct((B, 2 * N_HEADS, S), F32)],
        scratch_shapes=[pltpu.VMEM((8 + tile, 2 * d_mlstm), F32),
                        pltpu.VMEM((32 + tile, d_conv), F32)],
        compiler_params=pltpu.CompilerParams(dimension_semantics=("arbitrary", "arbitrary"),
                                             vmem_limit_bytes=VMEM_LIMIT),
        name="in_proj",
    )(x, nw, w_main, wgc, wgr, qkw, qkb, gbc, gbr, cw, cb, cnw, cnb)


def _mlstm_body(q_ref, k_ref, v_ref, o_ref, gcol_ref, grow_ref, nw_ref, y_ref,
                c_scr, n_scr, m_scr, *, tile):
    @pl.when(pl.program_id(1) == 0)
    def _():
        c_scr[...] = jnp.zeros(c_scr.shape, F32)
        n_scr[...] = jnp.zeros(n_scr.shape, F32)
        m_scr[...] = jnp.zeros(m_scr.shape, F32)

    ri = lax.broadcasted_iota(jnp.int32, (CHUNK, CHUNK), 0)
    ci = lax.broadcasted_iota(jnp.int32, (CHUNK, CHUNK), 1)
    causal = ci <= ri
    nt = (((1,), (1,)), ((), ()))
    tn = (((0,), (0,)), ((), ()))

    for c in range(tile // CHUNK):
        rows = slice(c * CHUNK, (c + 1) * CHUNK)
        for hd in range(N_HEADS):
            lanes = slice(hd * HEAD_DIM, (hd + 1) * HEAD_DIM)
            qh = q_ref[rows, lanes]
            kh = k_ref[rows, lanes]
            vh = v_ref[rows, lanes]
            i_col = gcol_ref[rows, hd:hd + 1]
            b_col = gcol_ref[rows, N_HEADS + hd:N_HEADS + hd + 1]
            i_row = grow_ref[hd:hd + 1, rows]
            b_row = grow_ref[N_HEADS + hd:N_HEADS + hd + 1, rows]
            c_st = c_scr[hd]
            n_st = n_scr[hd]
            m_st = m_scr[hd][:, 0:1]

            s_qk = lax.dot_general(qh, kh, nt, preferred_element_type=F32)
            d_mat = jnp.where(causal, b_col - b_row + i_row, -jnp.inf)
            inter = b_col + m_st
            m_row = jnp.maximum(inter, jnp.max(d_mat, axis=-1, keepdims=True))
            p = s_qk * jnp.exp(d_mat - m_row)
            w_inter = jnp.exp(inter - m_row)
            num = (w_inter * jnp.dot(qh, c_st.astype(BF16), preferred_element_type=F32)
                   + jnp.dot(p.astype(BF16), vh, preferred_element_type=F32))
            den = (w_inter * jnp.sum(qh.astype(F32) * n_st, axis=-1, keepdims=True)
                   + jnp.sum(p, axis=-1, keepdims=True))
            h_t = num / jnp.maximum(jnp.abs(den), jnp.exp(-m_row))

            mu = jnp.mean(h_t, axis=-1, keepdims=True)
            cen = h_t - mu
            var = jnp.mean(cen * cen, axis=-1, keepdims=True)
            h_n = cen * lax.rsqrt(var + EPS)
            y_ref[rows, lanes] = (_sigmoid(o_ref[rows, lanes]) * h_n * nw_ref[:, lanes]).astype(BF16)

            b_last = b_row[:, CHUNK - 1:CHUNK]
            g = b_last - b_col + i_col
            m_new = jnp.maximum(b_last + m_st, jnp.max(g, axis=0, keepdims=True))
            decay = jnp.exp(b_last + m_st - m_new)
            wk = jnp.exp(g - m_new) * kh.astype(F32)
            c_scr[hd] = decay * c_st + lax.dot_general(wk.astype(BF16), vh, tn,
                                                       preferred_element_type=F32)
            n_scr[hd] = decay * n_st + jnp.sum(wk, axis=0, keepdims=True)
            m_scr[hd] = jnp.broadcast_to(m_new, (1, HEAD_DIM))


def _mlstm(q, k, v, o, gcol, grow, nw):
    B, S, d_mlstm = q.shape
    tile = TILE_MLSTM
    tok = lambda width: pl.BlockSpec((None, tile, width), lambda b, s: (b, s, 0))
    return pl.pallas_call(
        functools.partial(_mlstm_body, tile=tile),
        grid=(B, S // tile),
        in_specs=[tok(d_mlstm), tok(d_mlstm), tok(d_mlstm), tok(d_mlstm), tok(2 * N_HEADS),
                  pl.BlockSpec((None, 2 * N_HEADS, tile), lambda b, s: (b, 0, s)),
                  _const_spec(nw.shape)],
        out_specs=tok(d_mlstm),
        out_shape=jax.ShapeDtypeStruct((B, S, d_mlstm), BF16),
        scratch_shapes=[pltpu.VMEM((N_HEADS, HEAD_DIM, HEAD_DIM), F32),
                        pltpu.VMEM((N_HEADS, 1, HEAD_DIM), F32),
                        pltpu.VMEM((N_HEADS, 1, HEAD_DIM), F32)],
        compiler_params=pltpu.CompilerParams(dimension_semantics=("arbitrary", "arbitrary"),
                                             vmem_limit_bytes=VMEM_LIMIT),
        name="mlstm",
    )(q, k, v, o, gcol, grow, nw)


def _out_mlp_body(x_ref, ym_ref, yc_ref, wo_ref, pmw_ref, plw_ref, w1_ref, w2_ref, pow_ref, out_ref,
                  *, d_mlstm, d_ff):
    y = (jnp.dot(ym_ref[...], wo_ref[0:d_mlstm, :], preferred_element_type=F32)
         + jnp.dot(yc_ref[...], wo_ref[d_mlstm:, :], preferred_element_type=F32))
    x1 = x_ref[...] + y * _rms_scale(y) * pmw_ref[...]
    h2 = (x1 * _rms_scale(x1) * plw_ref[...]).astype(BF16)
    acc = None
    for f in range(d_ff // FF_BLOCK):
        cols = slice(f * FF_BLOCK, (f + 1) * FF_BLOCK)
        a = jnp.maximum(jnp.dot(h2, w1_ref[:, cols], preferred_element_type=F32), 0.0)
        part = jnp.dot((a * a).astype(BF16), w2_ref[cols, :], preferred_element_type=F32)
        acc = part if acc is None else acc + part
    out_ref[...] = x1 + acc * _rms_scale(acc) * pow_ref[...]


def _out_mlp(x2d, ym, yc, wo, pmw, plw, w1, w2, pow_):
    T, D = x2d.shape
    d_mlstm = ym.shape[1]
    d_ff = w1.shape[1]
    tile = TILE_OUT
    tok = lambda width: pl.BlockSpec((tile, width), lambda t: (t, 0))
    return pl.pallas_call(
        functools.partial(_out_mlp_body, d_mlstm=d_mlstm, d_ff=d_ff),
        grid=(T // tile,),
        in_specs=[tok(D), tok(d_mlstm), tok(yc.shape[1])] + [_const_spec(a.shape) for a in
                                                             (wo, pmw, plw, w1, w2, pow_)],
        out_specs=tok(D),
        out_shape=jax.ShapeDtypeStruct((T, D), F32),
        compiler_params=pltpu.CompilerParams(dimension_semantics=("arbitrary",),
                                             vmem_limit_bytes=VMEM_LIMIT),
        name="out_mlp",
    )(x2d, ym, yc, wo, pmw, plw, w1, w2, pow_)


def kernel(x, pre_mix_norm_w, w_in, qk_conv_w, qk_conv_b, gate_bias, mlstm_norm_w, conv_w, conv_b,
           conv_norm_w, conv_norm_b, w_out, post_mix_norm_w, pre_mlp_norm_w, w_ff1, w_ff2,
           post_mlp_norm_w):
    B, S, D = x.shape
    depth = w_in.shape[0]
    d_mlstm = N_HEADS * HEAD_DIM
    n_gate = 2 * N_HEADS
    g0 = 4 * d_mlstm
    row = lambda a: a.reshape(1, -1).astype(F32)
    for layer in range(depth):
        wi = w_in[layer]
        w_main = jnp.concatenate([wi[:, :g0], wi[:, g0 + n_gate:]], axis=1).astype(BF16)
        w_gate = wi[:, g0:g0 + n_gate]
        wgc = jnp.pad(w_gate, ((0, 0), (0, 128 - n_gate))).astype(BF16)
        wgr = w_gate.T.astype(BF16)
        gb = gate_bias[layer].astype(F32)
        gbc = jnp.pad(gb, (0, 128 - n_gate)).reshape(1, 128)
        gbr = gb.reshape(n_gate, 1)
        q, k, v, o, yc, gcol, grow = _in_proj(
            x, row(pre_mix_norm_w[layer]), w_main, wgc, wgr,
            qk_conv_w[layer].astype(F32), row(qk_conv_b[layer]), gbc, gbr,
            conv_w[layer].astype(F32), row(conv_b[layer]),
            row(conv_norm_w[layer]), row(conv_norm_b[layer]))
        ym = _mlstm(q, k, v, o, gcol, grow, row(mlstm_norm_w[layer]))
        out = _out_mlp(x.reshape(B * S, D), ym.reshape(B * S, d_mlstm), yc.reshape(B * S, -1),
                       w_out[layer].astype(BF16), row(post_mix_norm_w[layer]),
                       row(pre_mlp_norm_w[layer]), w_ff1[layer].astype(BF16),
                       w_ff2[layer].astype(BF16), row(post_mlp_norm_w[layer]))
        x = out.reshape(B, S, D)
    return x
```

```python
import functools

import jax
import jax.numpy as jnp
from jax import lax
from jax.experimental import pallas as pl
from jax.experimental.pallas import tpu as pltpu

F32 = jnp.float32
BF16 = jnp.bfloat16

N_HEADS = 4
HEAD_DIM = 128
N_CONV_GROUPS = 4
CONV_GROUP_DIM = 128
CHUNK = 128
EPS = 1e-6

SUBLANES = 8
LANES = 128

TILE_IN = 512
TILE_MLSTM = 512
TILE_OUT = 512
COL_BLOCK = 256
CONV_ROWS = 64
FF_BLOCK = 1024
VMEM_LIMIT = 56 * 1024 * 1024

G_A, G_WI, G_ENEG, G_EK, G_R, G_DECAY = range(6)
N_GATE_ROWS = 6 * SUBLANES

NT_DIMS = (((1,), (1,)), ((), ()))
TN_DIMS = (((0,), (0,)), ((), ()))


def _sigmoid(x):
    return 1.0 / (1.0 + jnp.exp(-x))


def _log_sigmoid(x):
    return jnp.minimum(x, 0.0) - jnp.log1p(jnp.exp(-jnp.abs(x)))


def _rms_scale(x):
    return lax.rsqrt(jnp.mean(x * x, axis=-1, keepdims=True) + EPS)


def _const_spec(shape):
    return pl.BlockSpec(shape, lambda *_: (0,) * len(shape), pipeline_mode=pl.Buffered(1))


def _causal_depthwise(buf_ref, row0, n_rows, lanes, first_off, w_ref, bias):
    width = w_ref.shape[0]
    last_off = first_off + width - 1
    span = -(-last_off // SUBLANES) * SUBLANES
    win = buf_ref[row0:row0 + n_rows + span, lanes]
    acc = jnp.broadcast_to(bias, (n_rows, bias.shape[-1]))
    for phase in range(SUBLANES):
        offs = [o for o in range(first_off, last_off + 1) if o % SUBLANES == phase]
        if not offs:
            continue
        shifted = pltpu.roll(win, win.shape[0] - phase, 0) if phase else win
        for o in offs:
            j = o - first_off
            acc = acc + w_ref[j:j + 1, lanes] * shifted[o - phase:o - phase + n_rows]
    return acc


def _lane_scan(x, op, identity):
    lane = lax.broadcasted_iota(jnp.int32, x.shape, 1)
    d = 1
    while d < x.shape[1]:
        x = op(x, jnp.where(lane >= d, pltpu.roll(x, d, 1), identity))
        d *= 2
    return x


def _in_proj_body(x_ref, nw_ref, w_ref, wg_ref, gb_ref, qkw_ref, qkb_ref,
                  cw_ref, cb_ref, cnw_ref, cnb_ref,
                  q_ref, k_ref, v_ref, o_ref, yc_ref, g_ref,
                  qkbuf, ubuf, m_scr, *, tile, d_mlstm, d_conv):
    qk_halo = SUBLANES
    u_halo = 4 * SUBLANES
    qk_width = qkw_ref.shape[0]
    conv_width = cw_ref.shape[0]
    d_qk = 2 * d_mlstm

    @pl.when(pl.program_id(1) == 0)
    def _():
        qkbuf[0:qk_halo, :] = jnp.zeros((qk_halo, d_qk), F32)
        ubuf[0:u_halo, :] = jnp.zeros((u_halo, d_conv), F32)
        m_scr[...] = jnp.zeros(m_scr.shape, F32)

    x = x_ref[...]
    h = (x * _rms_scale(x) * nw_ref[...]).astype(BF16)

    def proj(c0, width=COL_BLOCK):
        return jnp.dot(h, w_ref[:, c0:c0 + width], preferred_element_type=F32)

    for cb in range(d_qk // COL_BLOCK):
        cols = slice(cb * COL_BLOCK, (cb + 1) * COL_BLOCK)
        qkbuf[qk_halo:qk_halo + tile, cols] = proj(cb * COL_BLOCK)
        for r in range(tile // 128):
            acc = _causal_depthwise(qkbuf, r * 128, 128, cols, qk_halo - (qk_width - 1),
                                    qkw_ref, qkb_ref[:, cols])
            qk = acc * _sigmoid(acc)
            rows = slice(r * 128, (r + 1) * 128)
            if cb * COL_BLOCK < d_mlstm:
                q_ref[rows, cols] = (qk * (HEAD_DIM ** -0.5)).astype(BF16)
            else:
                k_ref[rows, cb * COL_BLOCK - d_mlstm:(cb + 1) * COL_BLOCK - d_mlstm] = qk.astype(BF16)
    qkbuf[0:qk_halo, :] = qkbuf[tile:tile + qk_halo, :]

    for cb in range(d_mlstm // COL_BLOCK):
        cols = slice(cb * COL_BLOCK, (cb + 1) * COL_BLOCK)
        v_ref[:, cols] = proj(d_qk + cb * COL_BLOCK).astype(BF16)
        o_ref[:, cols] = proj(d_qk + d_mlstm + cb * COL_BLOCK)

    c_val = d_qk + 2 * d_mlstm
    for cb in range(d_conv // COL_BLOCK):
        cols = slice(cb * COL_BLOCK, (cb + 1) * COL_BLOCK)
        ubuf[u_halo:u_halo + tile, cols] = (proj(c_val + cb * COL_BLOCK)
                                            * _sigmoid(proj(c_val + d_conv + cb * COL_BLOCK)))
    for g in range(N_CONV_GROUPS):
        lanes = slice(g * CONV_GROUP_DIM, (g + 1) * CONV_GROUP_DIM)
        for r in range(tile // CONV_ROWS):
            acc = _causal_depthwise(ubuf, r * CONV_ROWS, CONV_ROWS, lanes, u_halo - (conv_width - 1),
                                    cw_ref, cb_ref[:, lanes])
            mu = jnp.mean(acc, axis=-1, keepdims=True)
            cen = acc - mu
            var = jnp.mean(cen * cen, axis=-1, keepdims=True)
            y = cen * lax.rsqrt(var + EPS) * cnw_ref[:, lanes] + cnb_ref[:, lanes]
            yc_ref[r * CONV_ROWS:(r + 1) * CONV_ROWS, lanes] = (y * _sigmoid(y)).astype(BF16)
    ubuf[0:u_halo, :] = ubuf[tile:tile + u_halo, :]

    gates = lax.dot_general(wg_ref[...], h, NT_DIMS, preferred_element_type=F32) + gb_ref[...]
    i_pre = gates[0:SUBLANES]
    log_f = _log_sigmoid(gates[SUBLANES:2 * SUBLANES])
    m_in = m_scr[...]
    for c in range(tile // CHUNK):
        sl = slice(c * CHUNK, (c + 1) * CHUNK)
        b = _lane_scan(log_f[:, sl], jnp.add, 0.0)
        r = i_pre[:, sl] - b
        pm = _lane_scan(r, jnp.maximum, -jnp.inf)
        a = jnp.maximum(m_in, pm)
        a_last = jnp.broadcast_to(a[:, CHUNK - 1:CHUNK], a.shape)
        b_last = jnp.broadcast_to(b[:, CHUNK - 1:CHUNK], b.shape)
        g_ref[G_A * SUBLANES:(G_A + 1) * SUBLANES, sl] = a
        g_ref[G_WI * SUBLANES:(G_WI + 1) * SUBLANES, sl] = jnp.exp(m_in - a)
        g_ref[G_ENEG * SUBLANES:(G_ENEG + 1) * SUBLANES, sl] = jnp.exp(-(b + a))
        g_ref[G_EK * SUBLANES:(G_EK + 1) * SUBLANES, sl] = jnp.exp(r - a_last)
        g_ref[G_R * SUBLANES:(G_R + 1) * SUBLANES, sl] = r
        g_ref[G_DECAY * SUBLANES:(G_DECAY + 1) * SUBLANES, sl] = jnp.exp(m_in - a_last)
        m_in = b_last + a_last
    m_scr[...] = m_in


def _in_proj(x, nw, w_main, wg, gb, qkw, qkb, cw, cb, cnw, cnb):
    B, S, D = x.shape
    d_mlstm = N_HEADS * HEAD_DIM
    d_conv = N_CONV_GROUPS * CONV_GROUP_DIM
    tile = TILE_IN
    tok = lambda width: pl.BlockSpec((None, tile, width), lambda b, s: (b, s, 0))
    body = functools.partial(_in_proj_body, tile=tile, d_mlstm=d_mlstm, d_conv=d_conv)
    return pl.pallas_call(
        body,
        grid=(B, S // tile),
        in_specs=[tok(D)] + [_const_spec(a.shape) for a in
                             (nw, w_main, wg, gb, qkw, qkb, cw, cb, cnw, cnb)],
        out_specs=[tok(d_mlstm), tok(d_mlstm), tok(d_mlstm), tok(d_mlstm), tok(d_conv),
                   pl.BlockSpec((None, N_GATE_ROWS, tile), lambda b, s: (b, 0, s))],
        out_shape=[jax.ShapeDtypeStruct((B, S, d_mlstm), BF16),
                   jax.ShapeDtypeStruct((B, S, d_mlstm), BF16),
                   jax.ShapeDtypeStruct((B, S, d_mlstm), BF16),
                   jax.ShapeDtypeStruct((B, S, d_mlstm), F32),
                   jax.ShapeDtypeStruct((B, S, d_conv), BF16),
                   jax.ShapeDtypeStruct((B, N_GATE_ROWS, S), F32)],
        scratch_shapes=[pltpu.VMEM((SUBLANES + tile, 2 * d_mlstm), F32),
                        pltpu.VMEM((4 * SUBLANES + tile, d_conv), F32),
                        pltpu.VMEM((SUBLANES, LANES), F32)],
        compiler_params=pltpu.CompilerParams(dimension_semantics=("arbitrary", "arbitrary"),
                                             vmem_limit_bytes=VMEM_LIMIT),
        name="in_proj",
    )(x, nw, w_main, wg, gb, qkw, qkb, cw, cb, cnw, cnb)


def _split_bf16(x):
    hi = x.astype(BF16)
    lo = (x - hi.astype(F32)).astype(BF16)
    return jnp.concatenate([hi, lo], axis=1)


def _mlstm_body(q_ref, k_ref, v_ref, o_ref, g_ref, nw_ref, y_ref, cn_scr, *, tile):
    @pl.when(pl.program_id(1) == 0)
    def _():
        cn_scr[...] = jnp.zeros(cn_scr.shape, F32)

    ri = lax.broadcasted_iota(jnp.int32, (CHUNK, CHUNK), 0)
    ci = lax.broadcasted_iota(jnp.int32, (CHUNK, CHUNK), 1)
    causal = ci <= ri
    ones_v = jnp.ones((CHUNK, HEAD_DIM), BF16)
    averager = jnp.full((2 * HEAD_DIM, HEAD_DIM), 1.0 / HEAD_DIM, BF16)
    heads = range(N_HEADS)
    head_lanes = [slice(hd * HEAD_DIM, (hd + 1) * HEAD_DIM) for hd in heads]
    cn = [cn_scr[hd] for hd in heads]

    def lane_mean(x):
        return jnp.dot(_split_bf16(x), averager, preferred_element_type=F32)

    for c in range(tile // CHUNK):
        rows = slice(c * CHUNK, (c + 1) * CHUNK)

        def g_row(quantity, hd):
            idx = quantity * SUBLANES + hd
            return jnp.broadcast_to(g_ref[idx:idx + 1, rows], (CHUNK, CHUNK))

        def g_col(quantity, hd):
            return g_row(quantity, hd).T

        def twice(x):
            return jnp.concatenate([x, x], axis=1)

        q = [q_ref[rows, head_lanes[hd]] for hd in heads]
        k = [k_ref[rows, head_lanes[hd]] for hd in heads]
        vo = [jnp.concatenate([v_ref[rows, head_lanes[hd]], ones_v], axis=1) for hd in heads]
        s_qk = [lax.dot_general(q[hd], k[hd], NT_DIMS, preferred_element_type=F32) for hd in heads]
        q_cn = [jnp.dot(q[hd], cn[hd].astype(BF16), preferred_element_type=F32) for hd in heads]
        p = [(s_qk[hd] * jnp.exp(jnp.where(causal, g_row(G_R, hd) - g_col(G_A, hd), -jnp.inf))
              ).astype(BF16) for hd in heads]
        p_vo = [jnp.dot(p[hd], vo[hd], preferred_element_type=F32) for hd in heads]
        nd = [twice(g_col(G_WI, hd)) * q_cn[hd] + p_vo[hd] for hd in heads]
        h_t = [nd[hd][:, :HEAD_DIM] / jnp.maximum(jnp.abs(nd[hd][:, HEAD_DIM:]), g_col(G_ENEG, hd))
               for hd in heads]
        cen = [h_t[hd] - lane_mean(h_t[hd]) for hd in heads]
        var = [lane_mean(cen[hd] * cen[hd]) for hd in heads]
        for hd in heads:
            h_n = cen[hd] * lax.rsqrt(var[hd] + EPS)
            y_ref[rows, head_lanes[hd]] = (_sigmoid(o_ref[rows, head_lanes[hd]]) * h_n
                                           * nw_ref[:, head_lanes[hd]]).astype(BF16)
        wk = [(g_col(G_EK, hd) * k[hd].astype(F32)).astype(BF16) for hd in heads]
        upd = [lax.dot_general(wk[hd], vo[hd], TN_DIMS, preferred_element_type=F32) for hd in heads]
        cn = [twice(g_row(G_DECAY, hd)) * cn[hd] + upd[hd] for hd in heads]

    for hd in heads:
        cn_scr[hd] = cn[hd]


def _mlstm(q, k, v, o, gates, nw):
    B, S, d_mlstm = q.shape
    tile = TILE_MLSTM
    tok = lambda width: pl.BlockSpec((None, tile, width), lambda b, s: (b, s, 0))
    return pl.pallas_call(
        functools.partial(_mlstm_body, tile=tile),
        grid=(B, S // tile),
        in_specs=[tok(d_mlstm), tok(d_mlstm), tok(d_mlstm), tok(d_mlstm),
                  pl.BlockSpec((None, N_GATE_ROWS, tile), lambda b, s: (b, 0, s)),
                  _const_spec(nw.shape)],
        out_specs=tok(d_mlstm),
        out_shape=jax.ShapeDtypeStruct((B, S, d_mlstm), BF16),
        scratch_shapes=[pltpu.VMEM((N_HEADS, HEAD_DIM, 2 * HEAD_DIM), F32)],
        compiler_params=pltpu.CompilerParams(dimension_semantics=("arbitrary", "arbitrary"),
                                             vmem_limit_bytes=VMEM_LIMIT),
        name="mlstm",
    )(q, k, v, o, gates, nw)


def _out_mlp_body(x_ref, ym_ref, yc_ref, wo_ref, pmw_ref, plw_ref, w1_ref, w2_ref, pow_ref, out_ref,
                  *, d_mlstm, d_ff):
    y = (jnp.dot(ym_ref[...], wo_ref[0:d_mlstm, :], preferred_element_type=F32)
         + jnp.dot(yc_ref[...], wo_ref[d_mlstm:, :], preferred_element_type=F32))
    x1 = x_ref[...] + y * _rms_scale(y) * pmw_ref[...]
    h2 = (x1 * _rms_scale(x1) * plw_ref[...]).astype(BF16)
    acc = None
    for f in range(d_ff // FF_BLOCK):
        cols = slice(f * FF_BLOCK, (f + 1) * FF_BLOCK)
        a = jnp.maximum(jnp.dot(h2, w1_ref[:, cols], preferred_element_type=F32), 0.0)
        part = jnp.dot((a * a).astype(BF16), w2_ref[cols, :], preferred_element_type=F32)
        acc = part if acc is None else acc + part
    out_ref[...] = x1 + acc * _rms_scale(acc) * pow_ref[...]


def _out_mlp(x2d, ym, yc, wo, pmw, plw, w1, w2, pow_):
    T, D = x2d.shape
    d_mlstm = ym.shape[1]
    d_ff = w1.shape[1]
    tile = TILE_OUT
    tok = lambda width: pl.BlockSpec((tile, width), lambda t: (t, 0))
    return pl.pallas_call(
        functools.partial(_out_mlp_body, d_mlstm=d_mlstm, d_ff=d_ff),
        grid=(T // tile,),
        in_specs=[tok(D), tok(d_mlstm), tok(yc.shape[1])] + [_const_spec(a.shape) for a in
                                                             (wo, pmw, plw, w1, w2, pow_)],
        out_specs=tok(D),
        out_shape=jax.ShapeDtypeStruct((T, D), F32),
        compiler_params=pltpu.CompilerParams(dimension_semantics=("arbitrary",),
                                             vmem_limit_bytes=VMEM_LIMIT),
        name="out_mlp",
    )(x2d, ym, yc, wo, pmw, plw, w1, w2, pow_)


def kernel(x, pre_mix_norm_w, w_in, qk_conv_w, qk_conv_b, gate_bias, mlstm_norm_w, conv_w, conv_b,
           conv_norm_w, conv_norm_b, w_out, post_mix_norm_w, pre_mlp_norm_w, w_ff1, w_ff2,
           post_mlp_norm_w):
    B, S, D = x.shape
    depth = w_in.shape[0]
    d_mlstm = N_HEADS * HEAD_DIM
    n_gate = 2 * N_HEADS
    g0 = 4 * d_mlstm
    row = lambda a: a.reshape(1, -1).astype(F32)
    for layer in range(depth):
        wi = w_in[layer]
        w_main = jnp.concatenate([wi[:, :g0], wi[:, g0 + n_gate:]], axis=1).astype(BF16)
        w_i = wi[:, g0:g0 + N_HEADS].T
        w_f = wi[:, g0 + N_HEADS:g0 + n_gate].T
        wg = jnp.concatenate([w_i, w_i, w_f, w_f], axis=0).astype(BF16)
        gb = gate_bias[layer].astype(F32)
        gb = jnp.concatenate([gb[:N_HEADS], gb[:N_HEADS], gb[N_HEADS:], gb[N_HEADS:]]).reshape(-1, 1)
        q, k, v, o, yc, gates = _in_proj(
            x, row(pre_mix_norm_w[layer]), w_main, wg, gb,
            qk_conv_w[layer].astype(F32), row(qk_conv_b[layer]),
            conv_w[layer].astype(F32), row(conv_b[layer]),
            row(conv_norm_w[layer]), row(conv_norm_b[layer]))
        ym = _mlstm(q, k, v, o, gates, row(mlstm_norm_w[layer]))
        out = _out_mlp(x.reshape(B * S, D), ym.reshape(B * S, d_mlstm), yc.reshape(B * S, -1),
                       w_out[layer].astype(BF16), row(post_mix_norm_w[layer]),
                       row(pre_mlp_norm_w[layer]), w_ff1[layer].astype(BF16),
                       w_ff2[layer].astype(BF16), row(post_mlp_norm_w[layer]))
        x = out.reshape(B, S, D)
    return x
```
